```python
import jax, jax.numpy as jnp
from jax import lax
import numpy as np

D_MODEL = 1024
BATCH = 8
SEQ = 2048
DEPTH = 1

D_MIX = D_MODEL
SB_HEAD_DIM = 64
SB_WIDTH = D_MIX // 2
SB_HEADS = SB_WIDTH // SB_HEAD_DIM
GDN_HEAD_DIM = 128
GDN_WIDTH = D_MIX - SB_WIDTH
GDN_HEADS = GDN_WIDTH // GDN_HEAD_DIM
CONV_WIDTH = 4
CHUNK = 64
BLOCK_Q = 128
EPS = 1e-6
IN_SPLITS = (SB_WIDTH, SB_WIDTH, SB_WIDTH, SB_WIDTH,
             GDN_WIDTH, GDN_WIDTH, GDN_WIDTH, GDN_WIDTH,
             GDN_HEADS, GDN_HEADS)
D_IN = sum(IN_SPLITS)

kernel_name = "hybrid_stickbreak_gated_deltanet_block"


def rmsnorm(x, w):
    x32 = x.astype(jnp.float32)
    y = x32 * lax.rsqrt(jnp.mean(x32 * x32, axis=-1, keepdims=True) + EPS)
    return (y * w.astype(jnp.float32)).astype(x.dtype)


def l2norm(x):
    x32 = x.astype(jnp.float32)
    return x32 * lax.rsqrt(jnp.sum(x32 * x32, axis=-1, keepdims=True) + EPS)


def causal_depthwise_conv(x, w):
    k_taps, t_len = w.shape[0], x.shape[1]
    xp = jnp.pad(x, ((0, 0), (k_taps - 1, 0), (0, 0)))
    out = xp[:, 0:t_len] * w[0]
    for i in range(1, k_taps):
        out = out + xp[:, i:i + t_len] * w[i]
    return out


def stick_breaking_attention(q, k, v):
    t_len, d = q.shape[2], q.shape[3]
    scale = float(1.0 / np.sqrt(d))
    outs = []
    for blk in range(t_len // BLOCK_Q):
        start, end = blk * BLOCK_Q, (blk + 1) * BLOCK_Q
        qb = q[:, :, start:end]
        kb, vb = k[:, :, :end], v[:, :, :end]
        z = jnp.einsum('bhqd,bhkd->bhqk', qb, kb).astype(jnp.float32) * scale
        t_idx = start + jnp.arange(BLOCK_Q)
        s_idx = jnp.arange(end)
        causal = s_idx[None, :] < t_idx[:, None]
        sp = jnp.where(causal, jax.nn.softplus(z), 0.0)
        tail = lax.cumsum(sp, axis=sp.ndim - 1, reverse=True) - sp
        attn = jnp.where(causal, jnp.exp(jax.nn.log_sigmoid(z) - tail), 0.0)
        outs.append(jnp.einsum('bhqk,bhkd->bhqd', attn.astype(v.dtype), vb))
    return jnp.concatenate(outs, axis=2)


def gated_delta_chunked(q, k, v, g, beta):
    out_dtype = v.dtype
    b, h, t_len, dk = q.shape
    dv = v.shape[-1]
    n_chunks = t_len // CHUNK
    q = q.astype(jnp.float32) * float(dk ** -0.5)
    k = k.astype(jnp.float32)
    v = v.astype(jnp.float32)
    resh = lambda a: a.reshape((b, h, n_chunks, CHUNK) + a.shape[3:])
    q, k, v, g, beta = resh(q), resh(k), resh(v), resh(g.astype(jnp.float32)), resh(beta.astype(jnp.float32))
    g = lax.cumsum(g, axis=g.ndim - 1)
    incl = jnp.tril(jnp.ones((CHUNK, CHUNK), dtype=bool))
    strict = jnp.tril(jnp.ones((CHUNK, CHUNK), dtype=bool), k=-1)
    decay = jnp.exp(jnp.where(incl, g[..., :, None] - g[..., None, :], -jnp.inf))
    k_beta = k * beta[..., None]
    v_beta = v * beta[..., None]
    a_mat = jnp.where(strict, jnp.einsum('bhnid,bhnjd->bhnij', k_beta, k) * decay, 0.0)
    eye = jnp.eye(CHUNK, dtype=jnp.float32)
    rhs = jnp.concatenate([v_beta, k_beta * jnp.exp(g)[..., None]], axis=-1)
    sol = lax.linalg.triangular_solve(a_mat + eye, rhs, left_side=True, lower=True,
                                      unit_diagonal=True)
    u, w = sol[..., :dv], sol[..., dv:]
    attn_intra = jnp.where(incl, jnp.einsum('bhnid,bhnjd->bhnij', q, k) * decay, 0.0)

    def step(state, inp):
        q_c, k_c, u_c, w_c, g_c, a_c = inp
        v_new = u_c - jnp.einsum('bhcd,bhde->bhce', w_c, state)
        o_c = jnp.einsum('bhcd,bhde->bhce', q_c * jnp.exp(g_c)[..., None], state) \
            + jnp.einsum('bhij,bhje->bhie', a_c, v_new)
        g_last = g_c[..., -1]
        state = state * jnp.exp(g_last)[..., None, None] + jnp.einsum(
            'bhcd,bhce->bhde', k_c * jnp.exp(g_last[..., None] - g_c)[..., None], v_new)
        return state, o_c

    mv = lambda a: jnp.moveaxis(a, 2, 0)
    state0 = jnp.zeros((b, h, dk, dv), dtype=jnp.float32)
    _, o = lax.scan(step, state0, (mv(q), mv(k), mv(u), mv(w), mv(g), mv(attn_intra)))
    o = jnp.moveaxis(o, 0, 2).reshape(b, h, t_len, dv)
    return o.astype(out_dtype)


def setup_inputs(seed: int = 0) -> dict:
    key = jax.random.key(seed)
    ks = jax.random.split(key, 10)
    f32 = jnp.float32
    x = jax.random.normal(ks[0], (BATCH, SEQ, D_MODEL), f32)
    norm1_w = 1.0 + 0.02 * jax.random.normal(ks[1], (DEPTH, D_MODEL), f32)
    w_in = jax.random.normal(ks[2], (DEPTH, D_MODEL, D_IN), f32) * D_MODEL ** -0.5
    sb_norm_w = 1.0 + 0.02 * jax.random.normal(ks[3], (DEPTH, SB_HEAD_DIM), f32)
    gdn_conv_w = jax.random.normal(ks[4], (DEPTH, CONV_WIDTH, 3 * GDN_WIDTH), f32) * CONV_WIDTH ** -0.5
    gdn_A_log = jnp.log(jax.random.uniform(ks[5], (DEPTH, GDN_HEADS), f32, 1.0, 16.0))
    dt = jnp.exp(jax.random.uniform(ks[6], (DEPTH, GDN_HEADS), f32, float(np.log(1e-3)), float(np.log(1e-1))))
    gdn_dt_bias = dt + jnp.log(-jnp.expm1(-dt))
    gdn_norm_w = 1.0 + 0.02 * jax.random.normal(ks[7], (DEPTH, GDN_HEAD_DIM), f32)
    w_out = jax.random.normal(ks[8], (DEPTH, D_MIX, D_MODEL), f32) * D_MIX ** -0.5
    final_norm_w = 1.0 + 0.02 * jax.random.normal(ks[9], (D_MODEL,), f32)
    return {"x": x, "norm1_w": norm1_w, "w_in": w_in, "sb_norm_w": sb_norm_w,
            "gdn_conv_w": gdn_conv_w, "gdn_A_log": gdn_A_log, "gdn_dt_bias": gdn_dt_bias,
            "gdn_norm_w": gdn_norm_w, "w_out": w_out, "final_norm_w": final_norm_w}


def reference(x, norm1_w, w_in, sb_norm_w, gdn_conv_w, gdn_A_log, gdn_dt_bias,
              gdn_norm_w, w_out, final_norm_w):
    b, t_len, _ = x.shape
    split_idx = [int(s) for s in np.cumsum(IN_SPLITS)[:-1]]
    to_heads = lambda a, n, d: a.reshape(b, t_len, n, d).transpose(0, 2, 1, 3)
    for layer in range(DEPTH):
        h = rmsnorm(x, norm1_w[layer])
        proj = jnp.einsum('btd,de->bte', h, w_in[layer])
        sb_q, sb_k, sb_v, sb_z, g_q, g_k, g_v, g_z, g_b, g_a = jnp.split(proj, split_idx, axis=-1)

        o_sb = stick_breaking_attention(to_heads(sb_q, SB_HEADS, SB_HEAD_DIM),
                                        to_heads(sb_k, SB_HEADS, SB_HEAD_DIM),
                                        to_heads(sb_v, SB_HEADS, SB_HEAD_DIM))
        o_sb = rmsnorm(o_sb.transpose(0, 2, 1, 3), sb_norm_w[layer]).reshape(b, t_len, SB_WIDTH)
        o_sb = o_sb * jax.nn.silu(sb_z)

        qkv = jax.nn.silu(causal_depthwise_conv(jnp.concatenate([g_q, g_k, g_v], axis=-1),
                                                gdn_conv_w[layer]))
        gq, gk, gv = jnp.split(qkv, 3, axis=-1)
        gq = l2norm(to_heads(gq, GDN_HEADS, GDN_HEAD_DIM)).astype(x.dtype)
        gk = l2norm(to_heads(gk, GDN_HEADS, GDN_HEAD_DIM)).astype(x.dtype)
        gv = to_heads(gv, GDN_HEADS, GDN_HEAD_DIM)
        beta = jax.nn.sigmoid(g_b.astype(jnp.float32)).transpose(0, 2, 1)
        decay = (-jnp.exp(gdn_A_log[layer].astype(jnp.float32))
                 * jax.nn.softplus(g_a.astype(jnp.float32) + gdn_dt_bias[layer].astype(jnp.float32)))
        decay = decay.transpose(0, 2, 1)
        o_gdn = gated_delta_chunked(gq, gk, gv, decay, beta)
        o_gdn = rmsnorm(o_gdn.transpose(0, 2, 1, 3), gdn_norm_w[layer]).reshape(b, t_len, GDN_WIDTH)
        o_gdn = o_gdn * jax.nn.silu(g_z)

        mixed = jnp.concatenate([o_sb, o_gdn], axis=-1)
        x = x + jnp.einsum('bte,ed->btd', mixed, w_out[layer])
    return rmsnorm(x, final_norm_w)
```

```python
import functools

import jax
import jax.numpy as jnp
from jax import lax
from jax.experimental import pallas as pl
from jax.experimental.pallas import tpu as pltpu

F32 = jnp.float32
BF16 = jnp.bfloat16
HIGHEST = lax.Precision.HIGHEST

EPS = 1e-6
LANES = 128
SB_HEAD_DIM = 64
GDN_HEAD_DIM = 128
CONV_WIDTH = 4
CHUNK = 64
CONV_PAD = 8
VMEM_LIMIT = 48 * 1024 * 1024

NT_DIMS = (((1,), (1,)), ((), ()))
TN_DIMS = (((0,), (0,)), ((), ()))


def _softplus(x):
    return jnp.maximum(x, 0.0) + jnp.log(1.0 + jnp.exp(-jnp.abs(x)))


def _silu(x):
    return x * (1.0 / (1.0 + jnp.exp(-x)))


def _sigmoid(x):
    return 1.0 / (1.0 + jnp.exp(-x))


def _bdot(a, b):
    return jnp.dot(a.astype(BF16), b.astype(BF16), preferred_element_type=F32)


def _in_proj_kernel(x_ref, nw_ref, w_ref, wba_ref, wbat_ref,
                    q_ref, k_ref, v_ref, zg_ref, gqkv_ref, gz_ref, ba_ref, bat_ref,
                    *, sbw, gw, q_scale):
    x = x_ref[...]
    ms = jnp.mean(x * x, axis=-1, keepdims=True)
    h = (x * lax.rsqrt(ms + EPS) * nw_ref[...]).astype(BF16)

    def proj(lo, hi):
        return jnp.dot(h, w_ref[:, lo:hi], preferred_element_type=F32)

    q_ref[...] = (proj(0, sbw) * q_scale).astype(BF16)
    k_ref[...] = proj(sbw, 2 * sbw).astype(BF16)
    v_ref[...] = proj(2 * sbw, 3 * sbw).astype(BF16)
    zg_ref[...] = proj(3 * sbw, 4 * sbw).astype(BF16)
    g0 = 4 * sbw
    gqkv_ref[...] = proj(g0, g0 + 3 * gw).astype(BF16)
    gz_ref[...] = proj(g0 + 3 * gw, g0 + 4 * gw).astype(BF16)
    ba_ref[...] = jnp.dot(h, wba_ref[...], preferred_element_type=F32)
    bat_ref[...] = lax.dot_general(wbat_ref[...], h, NT_DIMS, preferred_element_type=F32)


def _in_proj(x2, norm_w, w_main, w_ba, w_bat, *, sbw, gw, q_scale, tm):
    n, d = x2.shape
    wcols = w_main.shape[1]
    nba = w_bat.shape[0]
    const = lambda i: (0, 0)
    row = lambda i: (i, 0)
    out_shape = (
        jax.ShapeDtypeStruct((n, sbw), BF16), jax.ShapeDtypeStruct((n, sbw), BF16),
        jax.ShapeDtypeStruct((n, sbw), BF16), jax.ShapeDtypeStruct((n, sbw), BF16),
        jax.ShapeDtypeStruct((n, 3 * gw), BF16), jax.ShapeDtypeStruct((n, gw), BF16),
        jax.ShapeDtypeStruct((n, LANES), F32), jax.ShapeDtypeStruct((nba, n), F32),
    )
    out_specs = (
        pl.BlockSpec((tm, sbw), row), pl.BlockSpec((tm, sbw), row),
        pl.BlockSpec((tm, sbw), row), pl.BlockSpec((tm, sbw), row),
        pl.BlockSpec((tm, 3 * gw), row), pl.BlockSpec((tm, gw), row),
        pl.BlockSpec((tm, LANES), row), pl.BlockSpec((nba, tm), lambda i: (0, i)),
    )
    return pl.pallas_call(
        functools.partial(_in_proj_kernel, sbw=sbw, gw=gw, q_scale=q_scale),
        grid=(n // tm,),
        in_specs=[
            pl.BlockSpec((tm, d), row),
            pl.BlockSpec((1, d), const),
            pl.BlockSpec((d, wcols), const),
            pl.BlockSpec((d, LANES), const),
            pl.BlockSpec((nba, d), const),
        ],
        out_specs=out_specs,
        out_shape=out_shape,
        compiler_params=pltpu.CompilerParams(
            dimension_semantics=("arbitrary",), vmem_limit_bytes=VMEM_LIMIT),
        name="in_proj",
    )(x2, norm_w, w_main, w_ba, w_bat)


def _sb_attn_kernel(q_ref, k_ref, v_ref, zg_ref, nw_ref, o_ref, *, tq, tk, hd):
    qi = pl.program_id(2)
    q = q_ref[0]
    lane = lax.broadcasted_iota(jnp.int32, (tk, LANES), 1)
    head0 = lane < hd
    r2 = lax.broadcasted_iota(jnp.int32, (2 * tk, 2 * tk), 0)
    c2 = lax.broadcasted_iota(jnp.int32, (2 * tk, 2 * tk), 1)
    ubd = (((r2 // tk) == (c2 // tk)) & (r2 >= c2)).astype(BF16)
    nd = tq // tk

    def block(kb, carry0, carry1, acc, masked):
        ks = pl.multiple_of(kb * tk, tk)
        kblk = k_ref[0, pl.ds(ks, tk), :]
        vblk = v_ref[0, pl.ds(ks, tk), :]
        zero = jnp.zeros_like(kblk)
        kbd = jnp.concatenate([jnp.where(head0, kblk, zero), jnp.where(head0, zero, kblk)], axis=0)
        vbd = jnp.concatenate([jnp.where(head0, vblk, zero), jnp.where(head0, zero, vblk)], axis=0)
        z = lax.dot_general(q, kbd, NT_DIMS, preferred_element_type=F32)
        sp = _softplus(z)
        if masked:
            t_idx = qi * tq + lax.broadcasted_iota(jnp.int32, (tq, 2 * tk), 0)
            s_idx = ks + (lax.broadcasted_iota(jnp.int32, (tq, 2 * tk), 1) % tk)
            causal = s_idx < t_idx
            sp = jnp.where(causal, sp, 0.0)
        cum = jnp.dot(sp.astype(BF16), ubd, preferred_element_type=F32)
        tot0 = jnp.sum(sp[:, :tk], axis=-1, keepdims=True)
        tot1 = jnp.sum(sp[:, tk:], axis=-1, keepdims=True)
        cumc = cum + jnp.concatenate([carry0, carry1], axis=1)
        a = jnp.exp(z - cumc)
        if masked:
            a = jnp.where(causal, a, 0.0)
        acc = acc + jnp.dot(a.astype(BF16), vbd, preferred_element_type=F32)
        carry0 = carry0 + jnp.broadcast_to(tot0, (tq, tk))
        carry1 = carry1 + jnp.broadcast_to(tot1, (tq, tk))
        return carry0, carry1, acc

    carry0 = jnp.zeros((tq, tk), F32)
    carry1 = jnp.zeros((tq, tk), F32)
    acc = jnp.zeros((tq, LANES), F32)
    for d in reversed(range(nd)):
        carry0, carry1, acc = block(qi * nd + d, carry0, carry1, acc, True)

    def body(it, st):
        return block(qi * nd - 1 - it, *st, False)

    carry0, carry1, acc = lax.fori_loop(0, qi * nd, body, (carry0, carry1, acc))

    r = lax.broadcasted_iota(jnp.int32, (LANES, LANES), 0)
    c = lax.broadcasted_iota(jnp.int32, (LANES, LANES), 1)
    jbd = jnp.where((r // hd) == (c // hd), 1.0 / hd, 0.0).astype(F32)
    ms = jnp.dot(acc * acc, jbd, precision=HIGHEST, preferred_element_type=F32)
    y = acc * lax.rsqrt(ms + EPS) * nw_ref[...]
    o_ref[0] = (y * _silu(zg_ref[0].astype(F32))).astype(BF16)


def _sb_attn(q, k, v, zg, nw2, *, tq, tk, hd):
    b, t, w = q.shape
    qmap = lambda bi, p, i: (bi, i, p)
    kvmap = lambda bi, p, i: (bi, 0, p)
    return pl.pallas_call(
        functools.partial(_sb_attn_kernel, tq=tq, tk=tk, hd=hd),
        grid=(b, w // LANES, t // tq),
        in_specs=[
            pl.BlockSpec((1, tq, LANES), qmap),
            pl.BlockSpec((1, t, LANES), kvmap),
            pl.BlockSpec((1, t, LANES), kvmap),
            pl.BlockSpec((1, tq, LANES), qmap),
            pl.BlockSpec((1, LANES), lambda bi, p, i: (0, 0)),
        ],
        out_specs=pl.BlockSpec((1, tq, LANES), qmap),
        out_shape=jax.ShapeDtypeStruct((b, t, w), BF16),
        compiler_params=pltpu.CompilerParams(
            dimension_semantics=("arbitrary", "arbitrary", "arbitrary"),
            vmem_limit_bytes=VMEM_LIMIT),
        name="sb_attn",
    )(q, k, v, zg, nw2)


def _gdn_kernel(gqkv_ref, gz_ref, ba_ref, bat_ref, cw_ref, prow_ref, pcol_ref, gnw_ref,
                o_ref, xpad_ref, s_ref, *, nh, tb):
    gw = nh * GDN_HEAD_DIM
    step = pl.program_id(1)

    @pl.when(step == 0)
    def _():
        xpad_ref[0:CONV_PAD, :] = jnp.zeros((CONV_PAD, 3 * gw), F32)
        s_ref[...] = jnp.zeros_like(s_ref)

    xpad_ref[CONV_PAD:CONV_PAD + tb, :] = gqkv_ref[0].astype(F32)
    y = cw_ref[CONV_WIDTH - 1:CONV_WIDTH, :] * xpad_ref[CONV_PAD:CONV_PAD + tb, :]
    for i in range(CONV_WIDTH - 1):
        off = CONV_PAD - (CONV_WIDTH - 1) + i
        y = y + cw_ref[i:i + 1, :] * xpad_ref[off:off + tb, :]
    xpad_ref[0:CONV_PAD, :] = xpad_ref[tb:tb + CONV_PAD, :]
    qkv = _silu(y)

    r = lax.broadcasted_iota(jnp.int32, (tb, tb), 0)
    c = lax.broadcasted_iota(jnp.int32, (tb, tb), 1)
    same = (r // CHUNK) == (c // CHUNK)
    incl = same & (r >= c)
    strict = same & (r > c)
    m16 = (r // 16) == (c // 16)
    m32 = (r // 32) == (c // 32)

    ba = ba_ref[0]
    g_all = -jnp.exp(prow_ref[0:1, :]) * _softplus(ba + prow_ref[1:2, :])
    beta_all = _sigmoid(ba)
    gc_all = jnp.dot(incl.astype(F32), g_all, precision=HIGHEST, preferred_element_type=F32)
    gl_all = jnp.dot(same.astype(F32), g_all, precision=HIGHEST, preferred_element_type=F32)
    g_rows = -jnp.exp(pcol_ref[0]) * _softplus(bat_ref[...] + pcol_ref[1])
    gc_rows = jnp.dot(g_rows, (same & (r <= c)).astype(F32), precision=HIGHEST,
                      preferred_element_type=F32)

    dk = GDN_HEAD_DIM
    for h in range(nh):
        qh = qkv[:, h * dk:(h + 1) * dk]
        kh = qkv[:, gw + h * dk:gw + (h + 1) * dk]
        vh = qkv[:, 2 * gw + h * dk:2 * gw + (h + 1) * dk]
        qn = qh * (lax.rsqrt(jnp.sum(qh * qh, axis=-1, keepdims=True) + EPS) * (dk ** -0.5))
        kn = kh * lax.rsqrt(jnp.sum(kh * kh, axis=-1, keepdims=True) + EPS)
        beta = beta_all[:, h:h + 1]
        gcol = gc_all[:, nh + h:nh + h + 1]
        glcol = gl_all[:, nh + h:nh + h + 1]
        grow = gc_rows[nh + h:nh + h + 1, :]
        decay = jnp.exp(jnp.where(incl, gcol - grow, -jnp.inf))
        kb = kn * beta
        kn_b = kn.astype(BF16)
        kk = lax.dot_general(kb.astype(BF16), kn_b, NT_DIMS, preferred_element_type=F32)
        a_mat = jnp.where(strict, kk * decay, 0.0)
        qk = lax.dot_general(qn.astype(BF16), kn_b, NT_DIMS, preferred_element_type=F32)
        attn = qk * decay

        ad = jnp.where(m16, a_mat, 0.0)
        yv = -ad
        pw = _bdot(ad, ad)
        yv = yv + pw + _bdot(yv, pw)
        pw = _bdot(pw, pw)
        yv = yv + pw + _bdot(yv, pw)
        pw = _bdot(pw, pw)
        yv = yv + pw + _bdot(yv, pw)
        for lo_mask, hi_mask in ((m16, m32), (m32, same)):
            e = jnp.where(hi_mask & jnp.logical_not(lo_mask), a_mat, 0.0)
            f = e + _bdot(yv, e)
            yv = yv - (f + _bdot(f, yv))

        egc = jnp.exp(gcol)
        rhs = jnp.concatenate([vh * beta, kb * egc], axis=1)
        uw = rhs + _bdot(yv, rhs)
        aw = _bdot(attn, uw)
        ointra = aw[:, :dk]
        qeff = qn * egc - aw[:, dk:]
        kd = kn * jnp.exp(glcol - gcol)

        s = s_ref[h]
        outs = []
        for ci in range(tb // CHUNK):
            rows = slice(ci * CHUNK, (ci + 1) * CHUNK)
            mn = lax.dot_general(kd[rows].astype(BF16), uw[rows].astype(BF16), TN_DIMS,
                                 preferred_element_type=F32)
            sb = s.astype(BF16)
            outs.append(_bdot(qeff[rows], sb) + ointra[rows])
            egl = jnp.exp(glcol[ci * CHUNK:ci * CHUNK + 1, :])
            s = egl * s - _bdot(mn[:, dk:], sb) + mn[:, :dk]
        s_ref[h] = s
        o = jnp.concatenate(outs, axis=0)

        ms = jnp.mean(o * o, axis=-1, keepdims=True)
        yo = o * lax.rsqrt(ms + EPS) * gnw_ref[...]
        gate = gz_ref[0, :, h * dk:(h + 1) * dk].astype(F32)
        o_ref[0, :, h * dk:(h + 1) * dk] = (yo * _silu(gate)).astype(BF16)


def _gdn(gqkv, gz, ba, bat, conv_w, prow, pcol, gnw, *, nh, tb):
    b, t, w3 = gqkv.shape
    gw = w3 // 3
    nba = bat.shape[0]
    nblk = t // tb
    blk = lambda bi, i: (bi, i, 0)
    return pl.pallas_call(
        functools.partial(_gdn_kernel, nh=nh, tb=tb),
        grid=(b, nblk),
        in_specs=[
            pl.BlockSpec((1, tb, w3), blk),
            pl.BlockSpec((1, tb, gw), blk),
            pl.BlockSpec((1, tb, LANES), blk),
            pl.BlockSpec((nba, tb), lambda bi, i: (0, bi * nblk + i)),
            pl.BlockSpec((CONV_WIDTH, w3), lambda bi, i: (0, 0)),
            pl.BlockSpec((2, LANES), lambda bi, i: (0, 0)),
            pl.BlockSpec((2, nba, LANES), lambda bi, i: (0, 0, 0)),
            pl.BlockSpec((1, GDN_HEAD_DIM), lambda bi, i: (0, 0)),
        ],
        out_specs=pl.BlockSpec((1, tb, gw), blk),
        out_shape=jax.ShapeDtypeStruct((b, t, gw), BF16),
        scratch_shapes=[
            pltpu.VMEM((CONV_PAD + tb, w3), F32),
            pltpu.VMEM((nh, GDN_HEAD_DIM, GDN_HEAD_DIM), F32),
        ],
        compiler_params=pltpu.CompilerParams(
            dimension_semantics=("arbitrary", "arbitrary"), vmem_limit_bytes=VMEM_LIMIT),
        name="gdn",
    )(gqkv, gz, ba, bat, conv_w, prow, pcol, gnw)


def _out_proj_kernel(x_ref, osb_ref, ogdn_ref, wa_ref, wb_ref, fw_ref, o_ref, *, final_norm):
    acc = x_ref[...]
    acc = acc + jnp.dot(osb_ref[...], wa_ref[...], preferred_element_type=F32)
    acc = acc + jnp.dot(ogdn_ref[...], wb_ref[...], preferred_element_type=F32)
    if final_norm:
        ms = jnp.mean(acc * acc, axis=-1, keepdims=True)
        acc = acc * lax.rsqrt(ms + EPS) * fw_ref[...]
    o_ref[...] = acc


def _out_proj(x2, osb, ogdn, wa, wb, fw, *, final_norm, tm):
    n, d = x2.shape
    row = lambda i: (i, 0)
    const = lambda i: (0, 0)
    return pl.pallas_call(
        functools.partial(_out_proj_kernel, final_norm=final_norm),
        grid=(n // tm,),
        in_specs=[
            pl.BlockSpec((tm, d), row),
            pl.BlockSpec((tm, osb.shape[1]), row),
            pl.BlockSpec((tm, ogdn.shape[1]), row),
            pl.BlockSpec(wa.shape, const),
            pl.BlockSpec(wb.shape, const),
            pl.BlockSpec((1, d), const),
        ],
        out_specs=pl.BlockSpec((tm, d), row),
        out_shape=jax.ShapeDtypeStruct((n, d), F32),
        compiler_params=pltpu.CompilerParams(
            dimension_semantics=("arbitrary",), vmem_limit_bytes=VMEM_LIMIT),
        name="out_proj",
    )(x2, osb, ogdn, wa, wb, fw)


def _tiles(n, t):
    tm = 512 if n % 512 == 0 else LANES
    tq = 256 if t % 256 == 0 else LANES
    return tm, tq, LANES, 2 * CHUNK


def kernel(x, norm1_w, w_in, sb_norm_w, gdn_conv_w, gdn_A_log, gdn_dt_bias, gdn_norm_w, w_out,
           final_norm_w):
    b, t, d = x.shape
    depth = norm1_w.shape[0]
    sbw = d // 2
    gw = d - sbw
    nh = gw // GDN_HEAD_DIM
    n = b * t
    tm, tq, tk, tb = _tiles(n, t)
    assert sbw % LANES == 0 and gw % LANES == 0 and t % tb == 0 and n % tm == 0
    assert 2 * nh <= 8 and 2 * SB_HEAD_DIM == LANES
    nmain = 4 * sbw + 4 * gw
    q_scale = float(SB_HEAD_DIM) ** -0.5

    x2 = x.reshape(n, d)
    for layer in range(depth):
        w = w_in[layer]
        w_main = w[:, :nmain].astype(BF16)
        w_small = w[:, nmain:nmain + 2 * nh]
        w_ba = jnp.pad(w_small, ((0, 0), (0, LANES - 2 * nh))).astype(BF16)
        w_bat = w_small.T.astype(BF16)
        q, k, v, zg, gqkv, gz, ba, bat = _in_proj(
            x2, norm1_w[layer][None, :], w_main, w_ba, w_bat,
            sbw=sbw, gw=gw, q_scale=q_scale, tm=tm)

        as3 = lambda a: a.reshape(b, t, a.shape[-1])
        nw2 = jnp.tile(sb_norm_w[layer], LANES // SB_HEAD_DIM)[None, :]
        o_sb = _sb_attn(as3(q), as3(k), as3(v), as3(zg), nw2, tq=tq, tk=tk, hd=SB_HEAD_DIM)

        zpad = jnp.zeros((nh,), F32)
        prm = jnp.stack([jnp.concatenate([zpad, gdn_A_log[layer].astype(F32)]),
                         jnp.concatenate([zpad, gdn_dt_bias[layer].astype(F32)])])
        prow = jnp.pad(prm, ((0, 0), (0, LANES - 2 * nh)))
        pcol = jnp.broadcast_to(prm[:, :, None], (2, 2 * nh, LANES))
        o_gdn = _gdn(as3(gqkv), as3(gz), as3(ba), bat, gdn_conv_w[layer], prow, pcol,
                     gdn_norm_w[layer][None, :], nh=nh, tb=tb)

        wo = w_out[layer].astype(BF16)
        x2 = _out_proj(x2, o_sb.reshape(n, sbw), o_gdn.reshape(n, gw), wo[:sbw], wo[sbw:],
                       final_norm_w[None, :], final_norm=(layer == depth - 1), tm=tm)
    return x2.reshape(b, t, d)
```
